```python
import jax, jax.numpy as jnp
from jax import lax
import numpy as np

D_MODEL = 2048
BATCH = 4
SEQ = 2048
DEPTH = 1
DEC_BATCH = 128
DEC_SEQ = 1
PAST_LEN = 16384
PAGE_SIZE = 128

D_MIX = D_MODEL
D_A = D_MIX // 2
D_B = D_MIX - D_A
EXPAND = 128
H_A = D_A // EXPAND
DK = EXPAND
DV = D_A // H_A
CONV_W = 31
D_FF = ((8 * D_MODEL // 3 + 255) // 256) * 256
D_IN = 4 * D_A + 2 * D_B
N_MOD = 9
CHUNK = 32
HALF = 0.5
EPS = 1e-6

kernel_name = "hymba_hgrn2_conformer_macaron_adaln_step"

F32 = jnp.float32


def _rmsnorm(x, g):
    xf = x.astype(F32)
    y = xf * lax.rsqrt(jnp.mean(xf * xf, axis=-1, keepdims=True) + EPS)
    return (y * g.astype(F32)).astype(x.dtype)


def _layernorm(x, g, b):
    xf = x.astype(F32)
    mu = jnp.mean(xf, axis=-1, keepdims=True)
    xc = xf - mu
    var = jnp.mean(xc * xc, axis=-1, keepdims=True)
    return (xc * lax.rsqrt(var + EPS) * g.astype(F32) + b.astype(F32)).astype(x.dtype)


def _modulate(h, shift, scale):
    return h * (1 + scale[:, None, :]) + shift[:, None, :]


def _swiglu(h, w_up, w_down):
    a, b = jnp.split(h @ w_up, 2, axis=-1)
    return (jax.nn.silu(a) * b) @ w_down


def _hgrn2(q, k, v, logf, s0):
    bsz, L = q.shape[0], q.shape[1]
    c = CHUNK if L >= CHUNK else L
    n = -(-L // c)
    pad = n * c - L
    if pad:
        pw = ((0, 0), (0, pad), (0, 0), (0, 0))
        q, k, v, logf = [jnp.pad(t, pw) for t in (q, k, v, logf)]

    def blocks(t):
        return t.reshape(bsz, n, c, H_A, t.shape[-1]).transpose(1, 0, 3, 2, 4)

    causal = jnp.tril(jnp.ones((c, c), dtype=bool))[:, :, None]

    def step(S, blk):
        qc, kc, vc, lf = blk
        b = jnp.cumsum(lf, axis=2)
        o_inter = jnp.einsum('bhtk,bhkv->bhtv', qc * jnp.exp(b), S)
        diff = b[:, :, :, None, :] - b[:, :, None, :, :]
        decay = jnp.exp(jnp.where(causal, diff, -jnp.inf))
        att = jnp.einsum('bhtsk,bhsk->bhts', qc[:, :, :, None, :] * decay, kc)
        o = o_inter + jnp.einsum('bhts,bhsv->bhtv', att, vc)
        b_last = b[:, :, -1:, :]
        S_new = jnp.exp(b_last[:, :, 0, :])[..., None] * S + jnp.einsum(
            'bhsk,bhsv->bhkv', kc * jnp.exp(b_last - b), vc)
        return S_new, o

    S, o = lax.scan(step, s0, (blocks(q), blocks(k), blocks(v), blocks(logf)))
    o = o.transpose(1, 0, 3, 2, 4).reshape(bsz, n * c, H_A, DV)[:, :L]
    return o, S


def _dwconv(u_ext, w, b):
    y = lax.conv_general_dilated(
        u_ext, w[:, None, :].astype(u_ext.dtype), window_strides=(1,), padding='VALID',
        dimension_numbers=('NWC', 'WIO', 'NWC'), feature_group_count=u_ext.shape[-1])
    return y + b


def _layer(x, c, s_hgrn, conv_buf, lb, w_ada, b_ada, norm_g, w_f1_up, w_f1_down, w_in, b_in,
           g_norm_a, conv_w, conv_b, ln_g, ln_b, w_out, w_f2_up, w_f2_down):
    bsz, L, _ = x.shape
    mod = jax.nn.silu(c) @ w_ada + b_ada
    sh1, sc1, gt1, sh2, sc2, gt2, sh3, sc3, gt3 = jnp.split(mod, N_MOD, axis=-1)

    h = _modulate(_rmsnorm(x, norm_g[0]), sh1, sc1)
    x = x + HALF * gt1[:, None, :] * _swiglu(h, w_f1_up, w_f1_down)

    h = _modulate(_rmsnorm(x, norm_g[1]), sh2, sc2)
    z = h @ w_in + b_in
    zq, zf, zi, zg, za, zb = jnp.split(
        z, [D_A, 2 * D_A, 3 * D_A, 4 * D_A, 4 * D_A + D_B], axis=-1)

    def heads(t):
        return t.astype(F32).reshape(bsz, L, H_A, -1)
    lbh = lb.reshape(H_A, DK)
    zf_h = heads(zf)
    q = jax.nn.silu(heads(zq))
    logf = jnp.logaddexp(jnp.log(lbh), jnp.log1p(-lbh) + jax.nn.log_sigmoid(zf_h))
    k = (1 - lbh) * jax.nn.sigmoid(-zf_h)
    v = heads(zi)
    o, s_new = _hgrn2(q, k, v, logf, s_hgrn.astype(F32))
    o = o * lax.rsqrt(jnp.mean(o * o, axis=-1, keepdims=True) + EPS)
    o = (o.reshape(bsz, L, D_A) * g_norm_a.astype(F32)
         * jax.nn.silu(zg.astype(F32))).astype(x.dtype)

    u = za * jax.nn.sigmoid(zb)
    u_ext = jnp.concatenate([conv_buf.astype(u.dtype), u], axis=1)
    y = jax.nn.silu(_layernorm(_dwconv(u_ext, conv_w, conv_b), ln_g, ln_b))

    mix = jnp.concatenate([o, y.astype(x.dtype)], axis=-1) @ w_out
    x = x + gt2[:, None, :] * mix

    h = _modulate(_rmsnorm(x, norm_g[2]), sh3, sc3)
    x = x + HALF * gt3[:, None, :] * _swiglu(h, w_f2_up, w_f2_down)
    return x, s_new, u_ext[:, -(CONV_W - 1):]


def setup_inputs(seed: int = 0) -> dict:
    key = jax.random.key(seed)
    ks = iter(jax.random.split(key, 32))
    nrm = lambda shape, s: jax.random.normal(next(ks), shape, F32) * s
    D = D_MODEL
    return {
        "x_prompt": nrm((BATCH, SEQ, D), 1.0),
        "x_sample": nrm((DEC_BATCH, DEC_SEQ, D), 1.0),
        "state_hgrn": nrm((DEPTH, DEC_BATCH, H_A, DK, DV), 0.5),
        "state_conv": nrm((DEPTH, DEC_BATCH, CONV_W - 1, D_B), 0.5),
        "c_prompt": nrm((BATCH, D), 1.0),
        "c_sample": nrm((DEC_BATCH, D), 1.0),
        "lb_logits": nrm((DEPTH + 1, D_A), 0.5),
        "w_ada": nrm((DEPTH, D, N_MOD * D), 0.5 * D ** -0.5),
        "b_ada": nrm((DEPTH, N_MOD * D), 0.05),
        "norm_g": 1.0 + nrm((DEPTH, 3, D), 0.05),
        "w_f1_up": nrm((DEPTH, D, 2 * D_FF), D ** -0.5),
        "w_f1_down": nrm((DEPTH, D_FF, D), D_FF ** -0.5),
        "w_in": nrm((DEPTH, D, D_IN), D ** -0.5),
        "b_in": nrm((DEPTH, D_IN), 0.02),
        "g_norm_a": 1.0 + nrm((DEPTH, D_A), 0.05),
        "conv_w": nrm((DEPTH, CONV_W, D_B), CONV_W ** -0.5),
        "conv_b": nrm((DEPTH, D_B), 0.02),
        "ln_g": 1.0 + nrm((DEPTH, D_B), 0.05),
        "ln_b": nrm((DEPTH, D_B), 0.02),
        "w_out": nrm((DEPTH, D_MIX, D), D_MIX ** -0.5),
        "w_f2_up": nrm((DEPTH, D, 2 * D_FF), D ** -0.5),
        "w_f2_down": nrm((DEPTH, D_FF, D), D_FF ** -0.5),
        "final_g": 1.0 + nrm((D,), 0.05),
    }


def reference(x_prompt, x_sample, state_hgrn, state_conv, c_prompt, c_sample, lb_logits,
              w_ada, b_ada, norm_g, w_f1_up, w_f1_down, w_in, b_in, g_norm_a, conv_w, conv_b,
              ln_g, ln_b, w_out, w_f2_up, w_f2_down, final_g):
    lbs = jnp.cumsum(jax.nn.softmax(lb_logits.astype(F32), axis=0), axis=0)
    bp = x_prompt.shape[0]
    s0 = jnp.zeros((bp, H_A, DK, DV), F32)
    buf0 = jnp.zeros((bp, CONV_W - 1, D_B), x_prompt.dtype)
    xp, xs = x_prompt, x_sample
    hp, hs, cp, cs = [], [], [], []
    for l in range(DEPTH):
        wl = (lbs[l], w_ada[l], b_ada[l], norm_g[l], w_f1_up[l], w_f1_down[l], w_in[l], b_in[l],
              g_norm_a[l], conv_w[l], conv_b[l], ln_g[l], ln_b[l], w_out[l], w_f2_up[l],
              w_f2_down[l])
        xp, sp, bufp = _layer(xp, c_prompt, s0, buf0, *wl)
        xs, ss, bufs = _layer(xs, c_sample, state_hgrn[l], state_conv[l], *wl)
        hp.append(sp)
        hs.append(ss)
        cp.append(bufp)
        cs.append(bufs)
    y_prompt = _rmsnorm(xp, final_g)
    y_sample = _rmsnorm(xs, final_g)
    new_hgrn_prompt = jnp.stack(hp, axis=0).astype(state_hgrn.dtype)
    new_hgrn_sample = jnp.stack(hs, axis=0).astype(state_hgrn.dtype)
    new_conv_prompt = jnp.stack(cp, axis=0).astype(state_conv.dtype)
    new_conv_sample = jnp.stack(cs, axis=0).astype(state_conv.dtype)
    return (y_prompt, y_sample, new_hgrn_prompt, new_hgrn_sample, new_conv_prompt, new_conv_sample)
```

```python
import functools

import jax
import jax.numpy as jnp
from jax import lax
from jax.experimental import pallas as pl
from jax.experimental.pallas import tpu as pltpu

F32 = jnp.float32
BF16 = jnp.bfloat16

D_MODEL = 2048
D_A = 1024
D_B = 1024
H_A = 8
DK = 128
DV = 128
CONV_W = 31
D_FF = 5632
D_IN = 4 * D_A + 2 * D_B
N_MOD = 9
EPS = 1e-6

V7X_VMEM_BYTES = 64 * 1024 * 1024
VMEM_LIMIT_BYTES = V7X_VMEM_BYTES - 4 * 1024 * 1024
LANES = 128
SUBLANES = 8
BF16_ROWS = 16

ROW_CHUNK = BF16_ROWS
TM_PROMPT = 1024
TF = 256
TN = 512
TN_ADA = 1024

CHUNK = 32
HALF_CHUNK = CHUNK // 2
TT = 256
EXP_SAFE = 80.0
HALO = 32
CONV_ROWS = BF16_ROWS
N_SLAB = D_B // LANES


def _params(*sem):
    return pltpu.CompilerParams(dimension_semantics=sem, vmem_limit_bytes=VMEM_LIMIT_BYTES)


def _sigmoid(x):
    return 1.0 / (1.0 + jnp.exp(-x))


def _silu(x):
    return x * _sigmoid(x)


def _dot(a, b):
    return jnp.dot(a, b, preferred_element_type=F32)


def _ada_kernel(c_ref, w_ref, b_ref, o_ref):
    s = _silu(c_ref[...]).astype(BF16)
    o_ref[...] = _dot(s, w_ref[...].astype(BF16)) + b_ref[...]


def _ada(c_all, w_ada, b_ada):
    m = c_all.shape[0]
    n = w_ada.shape[1]
    return pl.pallas_call(
        _ada_kernel,
        grid=(n // TN_ADA,),
        in_specs=[
            pl.BlockSpec((m, D_MODEL), lambda j: (0, 0)),
            pl.BlockSpec((D_MODEL, TN_ADA), lambda j: (0, j)),
            pl.BlockSpec((1, TN_ADA), lambda j: (0, j)),
        ],
        out_specs=pl.BlockSpec((m, TN_ADA), lambda j: (0, j)),
        out_shape=jax.ShapeDtypeStruct((m, n), F32),
        compiler_params=_params("arbitrary"),
        name="ada",
    )(c_all, w_ada, b_ada.reshape(1, n))


def _norm_modulate(x_ref, g_ref, sh_ref, sc_ref, h_ref, tm, per_row):
    def body(r, carry):
        rows = pl.ds(pl.multiple_of(r * ROW_CHUNK, ROW_CHUNK), ROW_CHUNK)
        x = x_ref[rows, :]
        y = x * lax.rsqrt(jnp.mean(x * x, axis=-1, keepdims=True) + EPS) * g_ref[...]
        sc = sc_ref[rows, :] if per_row else sc_ref[...]
        sh = sh_ref[rows, :] if per_row else sh_ref[...]
        h_ref[rows, :] = (y * (1.0 + sc) + sh).astype(BF16)
        return carry

    lax.fori_loop(0, tm // ROW_CHUNK, body, 0)


def _mod_spec(rows, width, group_of, col_block):
    return pl.BlockSpec((None, rows, width), lambda i, j: (group_of(i), 0, col_block(j)))


def _ffn_kernel(*refs, tm, per_row, final):
    if final:
        x_ref, sh_ref, sc_ref, gt_ref, g_ref, wa_ref, wb_ref, wd_ref, fg_ref, o_ref, h_ref = refs
    else:
        x_ref, sh_ref, sc_ref, gt_ref, g_ref, wa_ref, wb_ref, wd_ref, o_ref, h_ref = refs
    j = pl.program_id(1)
    nj = pl.num_programs(1)

    @pl.when(j == 0)
    def _():
        _norm_modulate(x_ref, g_ref, sh_ref, sc_ref, h_ref, tm, per_row)

    h = h_ref[...]
    a = _dot(h, wa_ref[...].astype(BF16))
    b = _dot(h, wb_ref[...].astype(BF16))
    act = (_silu(a) * b).astype(BF16)
    wd = wd_ref[...].astype(BF16)

    @pl.when(j == 0)
    def _():
        for n in range(D_MODEL // TN):
            cols = slice(n * TN, (n + 1) * TN)
            o_ref[:, cols] = _dot(act, wd[:, cols])

    @pl.when(j > 0)
    def _():
        for n in range(D_MODEL // TN):
            cols = slice(n * TN, (n + 1) * TN)
            o_ref[:, cols] += _dot(act, wd[:, cols])

    @pl.when(j == nj - 1)
    def _():
        def body(r, carry):
            rows = pl.ds(pl.multiple_of(r * ROW_CHUNK, ROW_CHUNK), ROW_CHUNK)
            gt = gt_ref[rows, :] if per_row else gt_ref[...]
            xn = x_ref[rows, :] + 0.5 * gt * o_ref[rows, :]
            if final:
                xn = xn * lax.rsqrt(jnp.mean(xn * xn, axis=-1, keepdims=True) + EPS) * fg_ref[...]
            o_ref[rows, :] = xn
            return carry

        lax.fori_loop(0, tm // ROW_CHUNK, body, 0)


def _ffn(x, mod, mod_base, norm_g, g_idx, w_up, w_down, *, tm, per_row, rows_per_group, final_g=None):
    m = x.shape[0]
    nj = D_FF // TF
    final = final_g is not None
    r = tm if per_row else 1
    group_of = lambda i: (i * tm) // rows_per_group
    in_specs = [
        pl.BlockSpec((tm, D_MODEL), lambda i, j: (i, 0)),
        _mod_spec(r, D_MODEL, group_of, lambda j: mod_base),
        _mod_spec(r, D_MODEL, group_of, lambda j: mod_base + 1),
        _mod_spec(r, D_MODEL, group_of, lambda j: mod_base + 2),
        pl.BlockSpec((None, 1, D_MODEL), lambda i, j: (g_idx, 0, 0)),
        pl.BlockSpec((D_MODEL, TF), lambda i, j: (0, j)),
        pl.BlockSpec((D_MODEL, TF), lambda i, j: (0, nj + j)),
        pl.BlockSpec((TF, D_MODEL), lambda i, j: (j, 0)),
    ]
    args = [x, mod, mod, mod, norm_g, w_up, w_up, w_down]
    if final:
        in_specs.append(pl.BlockSpec((1, D_MODEL), lambda i, j: (0, 0)))
        args.append(final_g.reshape(1, D_MODEL))
    return pl.pallas_call(
        functools.partial(_ffn_kernel, tm=tm, per_row=per_row, final=final),
        grid=(m // tm, nj),
        in_specs=in_specs,
        out_specs=pl.BlockSpec((tm, D_MODEL), lambda i, j: (i, 0)),
        out_shape=jax.ShapeDtypeStruct((m, D_MODEL), F32),
        scratch_shapes=[pltpu.VMEM((tm, D_MODEL), BF16)],
        compiler_params=_params("parallel", "arbitrary"),
        name="ffn_final" if final else "ffn",
    )(*args)


def _inproj_kernel(x_ref, sh_ref, sc_ref, g_ref, w_ref, b_ref, z_ref, h_ref, *, tm, per_row):
    @pl.when(pl.program_id(1) == 0)
    def _():
        _norm_modulate(x_ref, g_ref, sh_ref, sc_ref, h_ref, tm, per_row)

    z_ref[...] = _dot(h_ref[...], w_ref[...].astype(BF16)) + b_ref[...]


def _inproj(x, mod, norm_g, w_in, b_in, *, tm, per_row, rows_per_group):
    m = x.shape[0]
    r = tm if per_row else 1
    group_of = lambda i: (i * tm) // rows_per_group
    return pl.pallas_call(
        functools.partial(_inproj_kernel, tm=tm, per_row=per_row),
        grid=(m // tm, D_IN // TN),
        in_specs=[
            pl.BlockSpec((tm, D_MODEL), lambda i, j: (i, 0)),
            _mod_spec(r, D_MODEL, group_of, lambda j: 3),
            _mod_spec(r, D_MODEL, group_of, lambda j: 4),
            pl.BlockSpec((None, 1, D_MODEL), lambda i, j: (1, 0, 0)),
            pl.BlockSpec((D_MODEL, TN), lambda i, j: (0, j)),
            pl.BlockSpec((1, TN), lambda i, j: (0, j)),
        ],
        out_specs=pl.BlockSpec((tm, TN), lambda i, j: (i, j)),
        out_shape=jax.ShapeDtypeStruct((m, D_IN), F32),
        scratch_shapes=[pltpu.VMEM((tm, D_MODEL), BF16)],
        compiler_params=_params("parallel", "arbitrary"),
        name="inproj",
    )(x, mod, mod, norm_g, w_in, b_in.reshape(1, D_IN))


def _outproj_kernel(oa_ref, yb_ref, wa_ref, wb_ref, x_ref, gt_ref, o_ref):
    mix = _dot(oa_ref[...], wa_ref[...].astype(BF16)) + _dot(yb_ref[...], wb_ref[...].astype(BF16))
    o_ref[...] = x_ref[...] + gt_ref[...] * mix


def _outproj(oa, yb, w_out, x, mod, *, tm, per_row, rows_per_group):
    m = x.shape[0]
    r = tm if per_row else 1
    group_of = lambda i: (i * tm) // rows_per_group
    gate_block = 5 * (D_MODEL // TN)
    return pl.pallas_call(
        _outproj_kernel,
        grid=(m // tm, D_MODEL // TN),
        in_specs=[
            pl.BlockSpec((tm, D_A), lambda i, j: (i, 0)),
            pl.BlockSpec((tm, D_B), lambda i, j: (i, 0)),
            pl.BlockSpec((D_A, TN), lambda i, j: (0, j)),
            pl.BlockSpec((D_B, TN), lambda i, j: (1, j)),
            pl.BlockSpec((tm, TN), lambda i, j: (i, j)),
            _mod_spec(r, TN, group_of, lambda j: gate_block + j),
        ],
        out_specs=pl.BlockSpec((tm, TN), lambda i, j: (i, j)),
        out_shape=jax.ShapeDtypeStruct((m, D_MODEL), F32),
        compiler_params=_params("parallel", "arbitrary"),
        name="outproj",
    )(oa, yb, w_out, w_out, x, mod)


def _lower_bound(lbl_ref):
    l = lbl_ref[...]
    e = jnp.exp(l - jnp.max(l, axis=0, keepdims=True))
    return e[0:1, :] / jnp.sum(e, axis=0, keepdims=True)


def _gates(zf, lb):
    e = jnp.exp(-jnp.abs(zf))
    r = 1.0 / (1.0 + e)
    pos = zf >= 0.0
    sig_p = jnp.where(pos, r, e * r)
    sig_n = jnp.where(pos, e * r, r)
    return lb + (1.0 - lb) * sig_p, (1.0 - lb) * sig_n


def _head_norm_gate(o, zg, gn):
    return o * lax.rsqrt(jnp.mean(o * o, axis=-1, keepdims=True) + EPS) * gn * _silu(zg)


def _hgrn_kernel(q_ref, f_ref, i_ref, g_ref, lbl_ref, gn_ref, o_ref, st_ref,
                 s_ref, qb_ref, lf_ref, kb_ref):
    t = pl.program_id(1)

    @pl.when(t == 0)
    def _():
        s_ref[...] = jnp.zeros_like(s_ref)

    lb = _lower_bound(lbl_ref)

    def pre(r, fmin):
        rows = pl.ds(pl.multiple_of(r * ROW_CHUNK, ROW_CHUNK), ROW_CHUNK)
        qb_ref[rows, :] = _silu(q_ref[rows, :])
        f, k = _gates(f_ref[rows, :], lb)
        lf_ref[rows, :] = jnp.log(f)
        kb_ref[rows, :] = k
        return jnp.minimum(fmin, jnp.min(f, axis=0, keepdims=True))

    fmin = lax.fori_loop(0, TT // ROW_CHUNK, pre, jnp.ones((1, D_A), F32))
    safe = jnp.min(jnp.log(fmin)) * HALF_CHUNK >= -EXP_SAFE

    row = lax.broadcasted_iota(jnp.int32, (CHUNK, CHUNK), 0)
    col = lax.broadcasted_iota(jnp.int32, (CHUNK, CHUNK), 1)
    causal = row >= col
    tri = causal.astype(F32)
    ones = jnp.ones((CHUNK, DK), F32)
    rowid = lax.broadcasted_iota(jnp.int32, (CHUNK, DK), 0)

    for h in range(H_A):
        cols = slice(h * DK, (h + 1) * DK)

        def chunk(c, s, cols=cols):
            rows = pl.ds(pl.multiple_of(c * CHUNK, CHUNK), CHUNK)
            q = qb_ref[rows, cols]
            lf = lf_ref[rows, cols]
            k = kb_ref[rows, cols]
            v = i_ref[rows, cols]
            b = jnp.dot(tri, lf, precision=lax.Precision.HIGHEST, preferred_element_type=F32)
            b_last = b[CHUNK - 1:CHUNK, :]
            b_mid = b[HALF_CHUNK - 1:HALF_CHUNK, :]
            o = _dot((q * jnp.exp(b)).astype(BF16), s.astype(BF16))

            def fast():
                qm = (q * jnp.exp(b - b_mid)).astype(BF16)
                km = (k * jnp.exp(b_mid - b)).astype(BF16)
                att = lax.dot_general(qm, km, (((1,), (1,)), ((), ())), preferred_element_type=F32)
                att = jnp.where(causal, att, 0.0).astype(BF16)
                return _dot(att, v.astype(BF16))

            def exact():
                acc = jnp.zeros((CHUNK, DV), F32)
                for si in range(CHUNK):
                    dec = jnp.exp(jnp.where(rowid >= si, b - b[si:si + 1, :], -jnp.inf))
                    a = jnp.sum(q * dec * k[si:si + 1, :], axis=-1, keepdims=True)
                    acc = acc + a * v[si:si + 1, :]
                return acc

            o = o + lax.cond(safe, fast, exact)
            kd = (k * jnp.exp(b_last - b)).astype(BF16)
            bl_col = lax.dot_general(lf, ones, (((0,), (0,)), ((), ())),
                                     precision=lax.Precision.HIGHEST, preferred_element_type=F32)
            s_new = jnp.exp(bl_col) * s + lax.dot_general(
                kd, v.astype(BF16), (((0,), (0,)), ((), ())), preferred_element_type=F32)
            o_ref[rows, cols] = _head_norm_gate(o, g_ref[rows, cols], gn_ref[:, cols]).astype(BF16)
            return s_new

        s_ref[h] = lax.fori_loop(0, TT // CHUNK, chunk, s_ref[h])

    @pl.when(t == pl.num_programs(1) - 1)
    def _():
        st_ref[...] = s_ref[...]


def _hgrn_prompt(z, lb_logits, g_norm_a, batch, seq):
    nt = seq // TT
    zspec = lambda cb: pl.BlockSpec((TT, D_A), lambda b, t, cb=cb: (b * nt + t, cb))
    return pl.pallas_call(
        _hgrn_kernel,
        grid=(batch, nt),
        in_specs=[
            zspec(0), zspec(1), zspec(2), zspec(3),
            pl.BlockSpec(lb_logits.shape, lambda b, t: (0, 0)),
            pl.BlockSpec((1, D_A), lambda b, t: (0, 0)),
        ],
        out_specs=[
            pl.BlockSpec((TT, D_A), lambda b, t: (b * nt + t, 0)),
            pl.BlockSpec((None, H_A, DK, DV), lambda b, t: (b, 0, 0, 0)),
        ],
        out_shape=[
            jax.ShapeDtypeStruct((batch * seq, D_A), BF16),
            jax.ShapeDtypeStruct((batch, H_A, DK, DV), F32),
        ],
        scratch_shapes=[
            pltpu.VMEM((H_A, DK, DV), F32),
            pltpu.VMEM((TT, D_A), F32),
            pltpu.VMEM((TT, D_A), F32),
            pltpu.VMEM((TT, D_A), F32),
        ],
        compiler_params=_params("parallel", "arbitrary"),
        name="hgrn_prompt",
    )(z, z, z, z, lb_logits, g_norm_a.reshape(1, D_A))


SAMPLE_BT = 8


def _hgrn_step_kernel(q_ref, f_ref, i_ref, g_ref, lbl_ref, gn_ref, s_ref, o_ref, so_ref):
    lb = _lower_bound(lbl_ref)
    q = _silu(q_ref[...])
    f, k = _gates(f_ref[...], lb)
    eye = (lax.broadcasted_iota(jnp.int32, (DK, DK), 0) == lax.broadcasted_iota(jnp.int32, (DK, DK), 1))

    def to_col(rowvec):
        return jnp.sum(jnp.where(eye, rowvec, 0.0), axis=1, keepdims=True)

    for bb in range(SAMPLE_BT):
        r = slice(bb, bb + 1)
        for h in range(H_A):
            cols = slice(h * DK, (h + 1) * DK)
            s_new = to_col(f[r, cols]) * s_ref[bb, h] + to_col(k[r, cols]) * i_ref[r, cols]
            so_ref[bb, h] = s_new
            o = jnp.sum(to_col(q[r, cols]) * s_new, axis=0, keepdims=True)
            o_ref[r, cols] = _head_norm_gate(o, g_ref[r, cols], gn_ref[:, cols])


def _hgrn_sample(z, lb_logits, g_norm_a, state):
    n = z.shape[0]
    zspec = lambda cb: pl.BlockSpec((SAMPLE_BT, D_A), lambda i, cb=cb: (i, cb))
    sspec = pl.BlockSpec((SAMPLE_BT, H_A, DK, DV), lambda i: (i, 0, 0, 0))
    return pl.pallas_call(
        _hgrn_step_kernel,
        grid=(n // SAMPLE_BT,),
        in_specs=[
            zspec(0), zspec(1), zspec(2), zspec(3),
            pl.BlockSpec(lb_logits.shape, lambda i: (0, 0)),
            pl.BlockSpec((1, D_A), lambda i: (0, 0)),
            sspec,
        ],
        out_specs=[pl.BlockSpec((SAMPLE_BT, D_A), lambda i: (i, 0)), sspec],
        out_shape=[
            jax.ShapeDtypeStruct((n, D_A), F32),
            jax.ShapeDtypeStruct(state.shape, F32),
        ],
        compiler_params=_params("parallel"),
        name="hgrn_sample",
    )(z, z, z, z, lb_logits, g_norm_a.reshape(1, D_A), state)


def _layernorm_swish(ys, lg_ref, lb_ref):
    tot = ys[0]
    for y in ys[1:]:
        tot = tot + y
    mu = jnp.sum(tot, axis=-1, keepdims=True) * (1.0 / D_B)
    xc = [y - mu for y in ys]
    sq = xc[0] * xc[0]
    for x in xc[1:]:
        sq = sq + x * x
    rstd = lax.rsqrt(jnp.sum(sq, axis=-1, keepdims=True) * (1.0 / D_B) + EPS)
    out = []
    for s, x in enumerate(xc):
        cols = slice(s * LANES, (s + 1) * LANES)
        out.append(_silu(x * rstd * lg_ref[:, cols] + lb_ref[:, cols]))
    return out


def _conv_kernel(za_ref, zb_ref, w_ref, cb_ref, lg_ref, lb_ref, y_ref, st_ref, ext_ref):
    t = pl.program_id(1)

    @pl.when(t == 0)
    def _():
        ext_ref[:, 0:HALO, :] = jnp.zeros((N_SLAB, HALO, LANES), F32)

    def glu(r, carry):
        r0 = pl.multiple_of(r * ROW_CHUNK, ROW_CHUNK)
        u = za_ref[pl.ds(r0, ROW_CHUNK), :] * _sigmoid(zb_ref[pl.ds(r0, ROW_CHUNK), :])
        for s in range(N_SLAB):
            ext_ref[s, pl.ds(HALO + r0, ROW_CHUNK), :] = u[:, s * LANES:(s + 1) * LANES]
        return carry

    lax.fori_loop(0, TT // ROW_CHUNK, glu, 0)

    for r in range(TT // CONV_ROWS):
        r0 = r * CONV_ROWS
        ys = []
        for s in range(N_SLAB):
            cols = slice(s * LANES, (s + 1) * LANES)
            acc = jnp.zeros((CONV_ROWS, LANES), F32) + cb_ref[:, cols]
            for j in range(CONV_W):
                first = r0 + HALO - (CONV_W - 1) + j
                acc = acc + w_ref[j:j + 1, cols] * ext_ref[s, first:first + CONV_ROWS, :]
            ys.append(acc)
        out = _layernorm_swish(ys, lg_ref, lb_ref)
        for s in range(N_SLAB):
            y_ref[r0:r0 + CONV_ROWS, s * LANES:(s + 1) * LANES] = out[s].astype(BF16)

    @pl.when(t == pl.num_programs(1) - 1)
    def _():
        for s in range(N_SLAB):
            st_ref[:, s * LANES:(s + 1) * LANES] = ext_ref[s, TT + HALO - (CONV_W - 1):TT + HALO, :]

    ext_ref[:, 0:HALO, :] = ext_ref[:, TT:TT + HALO, :]


def _conv_prompt(z, conv_w, conv_b, ln_g, ln_b, batch, seq):
    nt = seq // TT
    blk_a = 4 * D_A // D_B
    vec = lambda: pl.BlockSpec((1, D_B), lambda b, t: (0, 0))
    return pl.pallas_call(
        _conv_kernel,
        grid=(batch, nt),
        in_specs=[
            pl.BlockSpec((TT, D_B), lambda b, t: (b * nt + t, blk_a)),
            pl.BlockSpec((TT, D_B), lambda b, t: (b * nt + t, blk_a + 1)),
            pl.BlockSpec((CONV_W, D_B), lambda b, t: (0, 0)),
            vec(), vec(), vec(),
        ],
        out_specs=[
            pl.BlockSpec((TT, D_B), lambda b, t: (b * nt + t, 0)),
            pl.BlockSpec((None, CONV_W - 1, D_B), lambda b, t: (b, 0, 0)),
        ],
        out_shape=[
            jax.ShapeDtypeStruct((batch * seq, D_B), BF16),
            jax.ShapeDtypeStruct((batch, CONV_W - 1, D_B), F32),
        ],
        scratch_shapes=[pltpu.VMEM((N_SLAB, HALO + TT, LANES), F32)],
        compiler_params=_params("parallel", "arbitrary"),
        name="conv_prompt",
    )(z, z, conv_w, conv_b.reshape(1, D_B), ln_g.reshape(1, D_B), ln_b.reshape(1, D_B))


def _conv_step_kernel(za_ref, zb_ref, w_ref, cb_ref, lg_ref, lb_ref, st_ref, y_ref, so_ref):
    u = za_ref[...] * _sigmoid(zb_ref[...])
    w_past = w_ref[0:CONV_W - 1, :]
    rows = []
    for bb in range(SAMPLE_BT):
        st = st_ref[bb]
        rows.append(jnp.sum(st * w_past, axis=0, keepdims=True))
        so_ref[bb, 0:CONV_W - 2, :] = st_ref[bb, 1:CONV_W - 1, :]
        so_ref[bb, CONV_W - 2:CONV_W - 1, :] = u[bb:bb + 1, :]
    y = jnp.concatenate(rows, axis=0) + w_ref[CONV_W - 1:CONV_W, :] * u + cb_ref[...]
    mu = jnp.mean(y, axis=-1, keepdims=True)
    yc = y - mu
    var = jnp.mean(yc * yc, axis=-1, keepdims=True)
    y_ref[...] = _silu(yc * lax.rsqrt(var + EPS) * lg_ref[...] + lb_ref[...])


def _conv_sample(z, conv_w, conv_b, ln_g, ln_b, state):
    n = z.shape[0]
    blk_a = 4 * D_A // D_B
    vec = lambda: pl.BlockSpec((1, D_B), lambda i: (0, 0))
    sspec = pl.BlockSpec((SAMPLE_BT, CONV_W - 1, D_B), lambda i: (i, 0, 0))
    return pl.pallas_call(
        _conv_step_kernel,
        grid=(n // SAMPLE_BT,),
        in_specs=[
            pl.BlockSpec((SAMPLE_BT, D_B), lambda i: (i, blk_a)),
            pl.BlockSpec((SAMPLE_BT, D_B), lambda i: (i, blk_a + 1)),
            pl.BlockSpec((CONV_W, D_B), lambda i: (0, 0)),
            vec(), vec(), vec(),
            sspec,
        ],
        out_specs=[pl.BlockSpec((SAMPLE_BT, D_B), lambda i: (i, 0)), sspec],
        out_shape=[
            jax.ShapeDtypeStruct((n, D_B), F32),
            jax.ShapeDtypeStruct(state.shape, F32),
        ],
        compiler_params=_params("parallel"),
        name="conv_sample",
    )(z, z, conv_w, conv_b.reshape(1, D_B), ln_g.reshape(1, D_B), ln_b.reshape(1, D_B), state)


def _layer(x, mod, rows_per_group, tm, per_row, mixer, w):
    kw = dict(tm=tm, per_row=per_row, rows_per_group=rows_per_group)
    x = _ffn(x, mod, 0, w["norm_g"], 0, w["w_f1_up"], w["w_f1_down"], **kw)
    z = _inproj(x, mod, w["norm_g"], w["w_in"], w["b_in"], **kw)
    oa, yb, s_new, buf_new = mixer(z)
    x = _outproj(oa, yb, w["w_out"], x, mod, **kw)
    y = _ffn(x, mod, 6, w["norm_g"], 2, w["w_f2_up"], w["w_f2_down"], final_g=w["final_g"], **kw)
    return y, s_new, buf_new


def kernel(x_prompt, x_sample, state_hgrn, state_conv, c_prompt, c_sample, lb_logits, w_ada, b_ada,
           norm_g, w_f1_up, w_f1_down, w_in, b_in, g_norm_a, conv_w, conv_b, ln_g, ln_b, w_out,
           w_f2_up, w_f2_down, final_g):
    assert w_ada.shape[0] == 1, "single layer"
    bp, seq, _ = x_prompt.shape
    bs = x_sample.shape[0]
    w = dict(norm_g=norm_g[0].reshape(3, 1, D_MODEL), w_f1_up=w_f1_up[0], w_f1_down=w_f1_down[0],
             w_in=w_in[0], b_in=b_in[0], w_out=w_out[0], w_f2_up=w_f2_up[0], w_f2_down=w_f2_down[0],
             final_g=final_g)

    pad = (-(bp + bs)) % SUBLANES
    c_all = jnp.concatenate([c_prompt, c_sample, jnp.zeros((pad, D_MODEL), F32)], axis=0)
    mod = _ada(c_all, w_ada[0], b_ada[0])
    mod_p = mod[:bp].reshape(bp, 1, N_MOD * D_MODEL)
    mod_s = mod[bp:bp + bs].reshape(1, bs, N_MOD * D_MODEL)

    def mix_prompt(z):
        oa, s_new = _hgrn_prompt(z, lb_logits, g_norm_a[0], bp, seq)
        yb, buf_new = _conv_prompt(z, conv_w[0], conv_b[0], ln_g[0], ln_b[0], bp, seq)
        return oa, yb, s_new, buf_new

    def mix_sample(z):
        oa, s_new = _hgrn_sample(z, lb_logits, g_norm_a[0], state_hgrn[0])
        yb, buf_new = _conv_sample(z, conv_w[0], conv_b[0], ln_g[0], ln_b[0], state_conv[0])
        return oa.astype(BF16), yb.astype(BF16), s_new, buf_new

    yp, sp, bufp = _layer(x_prompt.reshape(bp * seq, D_MODEL), mod_p, seq, TM_PROMPT, False, mix_prompt, w)
    ys, ss, bufs = _layer(x_sample.reshape(bs, D_MODEL), mod_s, bs, bs, True, mix_sample, w)

    return (yp.reshape(bp, seq, D_MODEL), ys.reshape(bs, 1, D_MODEL),
            sp[None], ss[None], bufp[None], bufs[None])
```
